```python
import math
import jax
import jax.numpy as jnp
from jax import lax
import numpy as np

D_MODEL = 2048
BATCH = 16
SEQ = 256
DEPTH = 2
DEC_BATCH = 2
DEC_SEQ = 4096
PAST_LEN = 256

GRID_W = 64
CHUNK = 128
EPS = 1e-6
CONV_A_DIM = D_MODEL // 2
CONV_W = 3
SSD_INNER = D_MODEL
SSD_HEAD_DIM = 64
SSD_HEADS = SSD_INNER // SSD_HEAD_DIM
SSD_GROUPS = 4
SSD_STATE = 128
SSD_CONV_W = 3
SSD_XBC = SSD_INNER + 2 * SSD_GROUPS * SSD_STATE
SGU_DIM = D_MODEL // 2
SGU_GROUPS = 8
N_BRANCH = 3
IN_SPLITS = (CONV_A_DIM, CONV_A_DIM, CONV_A_DIM, SSD_INNER, SSD_XBC, 2 * SSD_HEADS, SGU_DIM, SGU_DIM, N_BRANCH * D_MODEL)
IN_DIM = 3 * CONV_A_DIM + SSD_INNER + SSD_XBC + 2 * SSD_HEADS + 2 * SGU_DIM + N_BRANCH * D_MODEL
E_GROUPS = 4
EXP_PER_GROUP = 4
N_EXP = E_GROUPS * EXP_PER_GROUP
TOP_K_IN_GROUP = 2
D_EXPERT = D_MODEL // 4
ADA_DIM = 6 * D_MODEL

kernel_name = 'hybrid_diffusion_step'


def rmsnorm(x, w):
    xf = x.astype(jnp.float32)
    y = xf * lax.rsqrt(jnp.mean(xf * xf, axis=-1, keepdims=True) + EPS)
    return (y * w.astype(jnp.float32)).astype(x.dtype)


def layernorm(x, g, b):
    xf = x.astype(jnp.float32)
    mu = jnp.mean(xf, axis=-1, keepdims=True)
    xc = xf - mu
    var = jnp.mean(xc * xc, axis=-1, keepdims=True)
    return (xc * lax.rsqrt(var + EPS) * g.astype(jnp.float32) + b.astype(jnp.float32)).astype(x.dtype)


def dwconv(x, w, b=None):
    k = w.shape[0]
    pad = (k - 1) // 2
    y = lax.conv_general_dilated(x, w[:, None, :].astype(x.dtype), window_strides=(1,), padding=[(pad, pad)],
                                 dimension_numbers=('NWC', 'WIO', 'NWC'), feature_group_count=x.shape[-1])
    if b is not None:
        y = y + b
    return y


def grid_pos_code(n_tok, dtype):
    rows = n_tok // GRID_W
    rr, cc = jnp.meshgrid(jnp.arange(rows, dtype=jnp.float32), jnp.arange(GRID_W, dtype=jnp.float32), indexing='ij')
    quarter = D_MODEL // 4
    omega = 1.0 / (10000.0 ** (jnp.arange(quarter, dtype=jnp.float32) / quarter))
    ar = rr.reshape(-1)[:, None] * omega
    ac = cc.reshape(-1)[:, None] * omega
    return jnp.concatenate([jnp.sin(ar), jnp.cos(ar), jnp.sin(ac), jnp.cos(ac)], axis=-1).astype(dtype)


def ssd_chunked(x, dt, a, bmat, cmat, h0):
    b, l, h, p = x.shape
    nc = l // CHUNK
    rep = h // bmat.shape[2]
    bh = jnp.repeat(bmat, rep, axis=2).reshape(b, nc, CHUNK, h, -1)
    ch = jnp.repeat(cmat, rep, axis=2).reshape(b, nc, CHUNK, h, -1)
    xdt = (x * dt[..., None]).reshape(b, nc, CHUNK, h, p)
    da = (dt * a).reshape(b, nc, CHUNK, h).transpose(0, 3, 1, 2)
    cs = jnp.cumsum(da, axis=-1)
    tri = jnp.tril(jnp.ones((CHUNK, CHUNK), dtype=bool))
    lmat = jnp.exp(jnp.where(tri, cs[..., :, None] - cs[..., None, :], -jnp.inf))
    scores = jnp.einsum('bcihn,bcjhn->bhcij', ch, bh) * lmat
    y_diag = jnp.einsum('bhcij,bcjhp->bcihp', scores, xdt)
    decay_st = jnp.exp(cs[..., -1:] - cs)
    st = jnp.einsum('bcqhn,bhcq,bcqhp->bchpn', bh, decay_st, xdt)
    st = jnp.concatenate([h0[:, None], st], axis=1)
    ccs = jnp.cumsum(jnp.pad(cs[..., -1], ((0, 0), (0, 0), (1, 0))), axis=-1)
    tri2 = jnp.tril(jnp.ones((nc + 1, nc + 1), dtype=bool))
    dchunk = jnp.exp(jnp.where(tri2, ccs[..., :, None] - ccs[..., None, :], -jnp.inf))
    hs = jnp.einsum('bhzc,bchpn->bzhpn', dchunk, st)
    y_off = jnp.einsum('bcihn,bchpn,bhci->bcihp', ch, hs[:, :-1], jnp.exp(cs))
    return (y_diag + y_off).reshape(b, l, h, p), hs[:, -1]


def mixer(h, pos, h0, p):
    b, L, _ = h.shape
    if pos is not None:
        h = h + pos
    proj = h @ p['w_in']
    split_points = np.cumsum(np.array(IN_SPLITS))[:-1].tolist()
    a_b, a_c, a_h, z, xbc, dt_raw, s_u, s_v, gates = jnp.split(proj, split_points, axis=-1)
    y_a = (a_b * dwconv(a_c * a_h, p['conv_a_w'])) @ p['w_out_a']
    xbc = jax.nn.silu(dwconv(xbc, p['ssd_conv_w'], p['ssd_conv_b']))
    xs, bm, cm = jnp.split(xbc, [SSD_INNER, SSD_INNER + SSD_GROUPS * SSD_STATE], axis=-1)
    f32 = jnp.float32
    xs4 = xs.reshape(b, L, SSD_HEADS, SSD_HEAD_DIM).astype(f32)
    bm = bm.reshape(b, L, SSD_GROUPS, SSD_STATE).astype(f32)
    cm = cm.reshape(b, L, SSD_GROUPS, SSD_STATE).astype(f32)
    dt = jax.nn.softplus(dt_raw.reshape(b, L, 2, SSD_HEADS).astype(f32) + p['ssd_dt_bias'].astype(f32))
    a = -jnp.exp(p['ssd_a_log'].astype(f32))
    h0f = h0.astype(f32)
    flip = lambda t: jnp.flip(t, axis=1)
    y_f, s_f = ssd_chunked(xs4, dt[:, :, 0], a[0], bm, cm, h0f[:, 0])
    y_r, s_b = ssd_chunked(flip(xs4), flip(dt[:, :, 1]), a[1], flip(bm), flip(cm), h0f[:, 1])
    y_s = y_f + flip(y_r) + p['ssd_d'].astype(f32)[:, None] * xs4
    y_s = y_s.reshape(b, L, SSD_INNER) * jax.nn.silu(z.astype(f32))
    yg = y_s.reshape(b, L, SSD_GROUPS, SSD_INNER // SSD_GROUPS)
    yg = yg * lax.rsqrt(jnp.mean(yg * yg, axis=-1, keepdims=True) + EPS)
    y_s = (yg.reshape(b, L, SSD_INNER) * p['ssd_norm_w'].astype(f32)).astype(h.dtype)
    y_b = y_s @ p['w_out_b']
    u = jax.nn.gelu(s_u)
    v = layernorm(jax.nn.gelu(s_v), p['sgu_ln_g'], p['sgu_ln_b'])
    v = v.reshape(b, L // CHUNK, CHUNK, SGU_GROUPS, SGU_DIM // SGU_GROUPS)
    sp = jnp.einsum('gij,bcjgd->bcigd', p['sgu_w'], v) + p['sgu_b'][:, :, None]
    y_c = (u * sp.reshape(b, L, SGU_DIM)) @ p['w_out_c']
    g_a, g_b, g_c = jnp.split(jax.nn.sigmoid(gates), N_BRANCH, axis=-1)
    out = (g_a * y_a + g_b * y_b + g_c * y_c) @ p['w_o']
    return out, jnp.stack([s_f, s_b], axis=1).astype(h0.dtype)


def hier_moe(h, p):
    b, L, D = h.shape
    t = h.reshape(-1, D)
    pg = jax.nn.softmax((t @ p['router_g_w'] + p['router_g_b']).astype(jnp.float32), axis=-1)
    gi = jnp.argmax(pg, axis=-1)
    pg_sel = jnp.max(pg, axis=-1)
    le = (t @ p['router_e_w'] + p['router_e_b']).astype(jnp.float32).reshape(-1, E_GROUPS, EXP_PER_GROUP)
    le_sel = jnp.einsum('ngk,ng->nk', le, jax.nn.one_hot(gi, E_GROUPS, dtype=jnp.float32))
    pe = jax.nn.softmax(le_sel, axis=-1)
    top_v, top_i = lax.top_k(pe, TOP_K_IN_GROUP)
    wts = pg_sel[:, None] * top_v / jnp.sum(top_v, axis=-1, keepdims=True)
    gidx = gi[:, None] * EXP_PER_GROUP + top_i
    comb = jnp.sum(jax.nn.one_hot(gidx, N_EXP, dtype=jnp.float32) * wts[..., None], axis=1).astype(h.dtype)
    hg = jnp.einsum('nd,edf->nef', t, p['exp_w_gate'])
    hu = jnp.einsum('nd,edf->nef', t, p['exp_w_up'])
    act = jax.nn.silu(hg) * hu * comb[:, :, None]
    return jnp.einsum('nef,efd->nd', act, p['exp_w_down']).reshape(b, L, D)


def layer(x, mod, pos, h0, p):
    sh1, sc1, g1, sh2, sc2, g2 = [m[:, None, :] for m in jnp.split(mod, 6, axis=-1)]
    h = rmsnorm(x, p['norm_mix_w']) * (1 + sc1) + sh1
    mix, st = mixer(h, pos, h0, p)
    x = x + g1 * mix
    h = rmsnorm(x, p['norm_ffn_w']) * (1 + sc2) + sh2
    x = x + g2 * hier_moe(h, p)
    return x, st


def setup_inputs(seed: int = 0) -> dict:
    key = jax.random.key(seed)
    ks = iter(jax.random.split(key, 40))
    f32 = jnp.float32
    D = D_MODEL
    L = DEPTH

    def nrm(shape, scale):
        return scale * jax.random.normal(next(ks), shape, f32)

    x_prompt = nrm((BATCH, SEQ, D), 1.0)
    x_sample = nrm((DEC_BATCH, DEC_SEQ, D), 1.0)
    state_ssd = nrm((DEC_BATCH, DEPTH, 2, SSD_HEADS, SSD_HEAD_DIM, SSD_STATE), 0.1)
    c = nrm((DEC_BATCH, D), 1.0)
    c_ctx = nrm((D,), 1.0)
    norm_mix_w = 1.0 + nrm((L, D), 0.02)
    norm_ffn_w = 1.0 + nrm((L, D), 0.02)
    w_ada = nrm((L, D, ADA_DIM), 0.5 * D ** -0.5)
    b_ada = nrm((L, ADA_DIM), 0.02)
    w_in = nrm((L, D, IN_DIM), D ** -0.5)
    conv_a_w = nrm((L, CONV_W, CONV_A_DIM), CONV_W ** -0.5)
    w_out_a = nrm((L, CONV_A_DIM, D), CONV_A_DIM ** -0.5)
    ssd_conv_w = nrm((L, SSD_CONV_W, SSD_XBC), SSD_CONV_W ** -0.5)
    ssd_conv_b = nrm((L, SSD_XBC), 0.02)
    ssd_a_log = jnp.log(jax.random.uniform(next(ks), (L, 2, SSD_HEADS), f32, 1.0, 16.0))
    dt0 = jnp.exp(jax.random.uniform(next(ks), (L, 2, SSD_HEADS), f32, math.log(1e-3), math.log(1e-1)))
    ssd_dt_bias = dt0 + jnp.log(-jnp.expm1(-dt0))
    ssd_d = 1.0 + nrm((L, SSD_HEADS), 0.02)
    ssd_norm_w = 1.0 + nrm((L, SSD_INNER), 0.02)
    w_out_b = nrm((L, SSD_INNER, D), SSD_INNER ** -0.5)
    sgu_ln_g = 1.0 + nrm((L, SGU_DIM), 0.02)
    sgu_ln_b = nrm((L, SGU_DIM), 0.02)
    sgu_w = nrm((L, SGU_GROUPS, CHUNK, CHUNK), CHUNK ** -0.5)
    sgu_b = 1.0 + nrm((L, CHUNK, SGU_GROUPS), 0.02)
    w_out_c = nrm((L, SGU_DIM, D), SGU_DIM ** -0.5)
    w_o = nrm((L, D, D), D ** -0.5)
    router_g_w = nrm((L, D, E_GROUPS), D ** -0.5)
    router_g_b = nrm((L, E_GROUPS), 0.01)
    router_e_w = nrm((L, D, N_EXP), D ** -0.5)
    router_e_b = nrm((L, N_EXP), 0.01)
    exp_w_gate = nrm((L, N_EXP, D, D_EXPERT), D ** -0.5)
    exp_w_up = nrm((L, N_EXP, D, D_EXPERT), D ** -0.5)
    exp_w_down = nrm((L, N_EXP, D_EXPERT, D), D_EXPERT ** -0.5)
    final_norm_w = 1.0 + nrm((D,), 0.02)
    return {'x_prompt': x_prompt, 'x_sample': x_sample, 'state_ssd': state_ssd, 'c': c, 'c_ctx': c_ctx,
            'norm_mix_w': norm_mix_w, 'norm_ffn_w': norm_ffn_w, 'w_ada': w_ada, 'b_ada': b_ada, 'w_in': w_in,
            'conv_a_w': conv_a_w, 'w_out_a': w_out_a, 'ssd_conv_w': ssd_conv_w, 'ssd_conv_b': ssd_conv_b,
            'ssd_a_log': ssd_a_log, 'ssd_dt_bias': ssd_dt_bias, 'ssd_d': ssd_d, 'ssd_norm_w': ssd_norm_w,
            'w_out_b': w_out_b, 'sgu_ln_g': sgu_ln_g, 'sgu_ln_b': sgu_ln_b, 'sgu_w': sgu_w, 'sgu_b': sgu_b,
            'w_out_c': w_out_c, 'w_o': w_o, 'router_g_w': router_g_w, 'router_g_b': router_g_b,
            'router_e_w': router_e_w, 'router_e_b': router_e_b, 'exp_w_gate': exp_w_gate, 'exp_w_up': exp_w_up,
            'exp_w_down': exp_w_down, 'final_norm_w': final_norm_w}


def reference(x_prompt, x_sample, state_ssd, c, c_ctx, norm_mix_w, norm_ffn_w, w_ada, b_ada, w_in, conv_a_w, w_out_a,
              ssd_conv_w, ssd_conv_b, ssd_a_log, ssd_dt_bias, ssd_d, ssd_norm_w, w_out_b, sgu_ln_g, sgu_ln_b,
              sgu_w, sgu_b, w_out_c, w_o, router_g_w, router_g_b, router_e_w, router_e_b, exp_w_gate, exp_w_up,
              exp_w_down, final_norm_w):
    pos = grid_pos_code(x_sample.shape[1], x_sample.dtype)
    zero_state = jnp.zeros((x_prompt.shape[0], 2, SSD_HEADS, SSD_HEAD_DIM, SSD_STATE), x_prompt.dtype)
    xc = x_prompt
    xl = x_sample
    new_states = []
    for l in range(DEPTH):
        p = dict(norm_mix_w=norm_mix_w[l], norm_ffn_w=norm_ffn_w[l], w_in=w_in[l], conv_a_w=conv_a_w[l],
                 w_out_a=w_out_a[l], ssd_conv_w=ssd_conv_w[l], ssd_conv_b=ssd_conv_b[l], ssd_a_log=ssd_a_log[l],
                 ssd_dt_bias=ssd_dt_bias[l], ssd_d=ssd_d[l], ssd_norm_w=ssd_norm_w[l], w_out_b=w_out_b[l],
                 sgu_ln_g=sgu_ln_g[l], sgu_ln_b=sgu_ln_b[l], sgu_w=sgu_w[l], sgu_b=sgu_b[l], w_out_c=w_out_c[l],
                 w_o=w_o[l], router_g_w=router_g_w[l], router_g_b=router_g_b[l], router_e_w=router_e_w[l],
                 router_e_b=router_e_b[l], exp_w_gate=exp_w_gate[l], exp_w_up=exp_w_up[l],
                 exp_w_down=exp_w_down[l])
        mod_ctx = jax.nn.silu(c_ctx)[None, :] @ w_ada[l] + b_ada[l]
        mod_lat = jax.nn.silu(c) @ w_ada[l] + b_ada[l]
        xc, st = layer(xc, mod_ctx, None, zero_state, p)
        new_states.append(st)
        xl, _ = layer(xl, mod_lat, pos, state_ssd[:, l], p)
    y_prompt = rmsnorm(xc, final_norm_w)
    y_sample = rmsnorm(xl, final_norm_w)
    new_state_ssd = jnp.stack(new_states, axis=1)
    return (y_prompt, y_sample, new_state_ssd)
```

```python
import functools

import jax
import jax.numpy as jnp
from jax import lax
from jax.experimental import pallas as pl
from jax.experimental.pallas import tpu as pltpu

F32 = jnp.float32
BF16 = jnp.bfloat16

D = 2048
CHUNK = 128
EPS = 1e-6
CONV_A = 1024
SSD_INNER = 2048
HEADS = 32
HEAD_DIM = 64
GROUPS = 4
STATE = 128
XBC = SSD_INNER + 2 * GROUPS * STATE
SGU = 1024
SGU_GROUPS = 8
E_GROUPS = 4
EXP_PER_GROUP = 4
N_EXP = 16
D_EXPERT = 512
GRID_W = 64
LANE = 128
SUBLANE = 8

COL_XBC = 0
COL_AB, COL_AC, COL_AH = 3072, 4096, 5120
COL_Z = 6144
COL_SU, COL_SV = 8192, 9216
COL_GATE = 10240
N_MAIN = 16384

VMEM_LIMIT = 56 * 1024 * 1024
NEG_BIG = -1e30


def _sigmoid(x):
    return 1.0 / (1.0 + jnp.exp(-x))


def _silu(x):
    return x * _sigmoid(x)


def _gelu_tanh(x):
    return 0.5 * x * (1.0 + jnp.tanh(0.7978845608028654 * (x + 0.044715 * (x * x * x))))


def _softplus(x):
    return jnp.maximum(x, 0.0) + jnp.log1p(jnp.exp(-jnp.abs(x)))


def _params(sem):
    return pltpu.CompilerParams(dimension_semantics=sem, vmem_limit_bytes=VMEM_LIMIT)


class _Geom:
    def __init__(self, n_ctx, ctx_len, n_lat, lat_len):
        self.n_ctx, self.ctx_len, self.n_lat, self.lat_len = n_ctx, ctx_len, n_lat, lat_len
        self.t_ctx = n_ctx * ctx_len
        self.t_lat = n_lat * lat_len
        self.t = self.t_ctx + self.t_lat
        assert ctx_len % CHUNK == 0 and lat_len % CHUNK == 0

    def mod_row(self, row):
        return jnp.where(row < self.t_ctx, 0, 1 + (row - self.t_ctx) // self.lat_len)

    def seq_pos(self, row):
        is_ctx = row < self.t_ctx
        pos = jnp.where(is_ctx, row % self.ctx_len, (row - self.t_ctx) % self.lat_len)
        length = jnp.where(is_ctx, self.ctx_len, self.lat_len)
        return pos, length

    def halo_flags(self, row0, rows):
        pos, length = self.seq_pos(row0)
        has_prev = jnp.where(pos == 0, 0.0, 1.0)
        has_next = jnp.where(pos + rows == length, 0.0, 1.0)
        return has_prev, has_next

    def halo_specs(self, rows, width, col_blk):
        per = rows // SUBLANE
        last = self.t // SUBLANE - 1
        prev = pl.BlockSpec((SUBLANE, width), lambda i, *_: (jnp.maximum(i * per - 1, 0), col_blk))
        nxt = pl.BlockSpec((SUBLANE, width), lambda i, *_: (jnp.minimum((i + 1) * per, last), col_blk))
        return prev, nxt


def _ada_kernel(c_ref, w_ref, b_ref, o_ref):
    s = _silu(c_ref[...]).astype(BF16)
    o_ref[...] = jnp.dot(s, w_ref[...].astype(BF16), preferred_element_type=F32) + b_ref[...]


def _ada(c_all, w_ada, b_ada):
    depth = w_ada.shape[0]
    n = w_ada.shape[2]
    tn = 1024
    return pl.pallas_call(
        _ada_kernel,
        grid=(depth, n // tn),
        in_specs=[
            pl.BlockSpec((SUBLANE, D), lambda l, j: (0, 0)),
            pl.BlockSpec((None, D, tn), lambda l, j: (l, 0, j)),
            pl.BlockSpec((None, 1, tn), lambda l, j: (l, 0, j)),
        ],
        out_specs=pl.BlockSpec((None, SUBLANE, tn), lambda l, j: (l, 0, j)),
        out_shape=jax.ShapeDtypeStruct((depth, SUBLANE, n), F32),
        compiler_params=_params(("arbitrary", "arbitrary")),
        name="ada",
    )(c_all, w_ada, b_ada.reshape(depth, 1, n))


def _inproj_kernel(x_ref, mod_ref, nw_ref, pos_ref, w_ref, wdt_ref, o_ref, dt_ref, hb_ref, *, tm, t_ctx):
    i = pl.program_id(0)
    j = pl.program_id(1)

    @pl.when(j == 0)
    def _():
        x = x_ref[...]
        ms = jnp.mean(x * x, axis=-1, keepdims=True)
        y = x * lax.rsqrt(ms + EPS) * nw_ref[...]
        h = y * (1.0 + mod_ref[1:2, :]) + mod_ref[0:1, :]
        is_lat = jnp.where(i * tm >= t_ctx, 1.0, 0.0)
        h = h + pos_ref[...] * is_lat
        hb = h.astype(BF16)
        hb_ref[...] = hb
        dt_ref[...] = jnp.dot(hb, wdt_ref[...], preferred_element_type=F32)

    o_ref[...] = jnp.dot(hb_ref[...], w_ref[...], preferred_element_type=F32)


def _in_proj(geom, x, mod, norm_w, pos, w_main, w_dt):
    tm, tn = 512, 1024
    assert geom.t_ctx % tm == 0 and geom.lat_len % tm == 0
    kern = functools.partial(_inproj_kernel, tm=tm, t_ctx=geom.t_ctx)

    def pos_map(i, j):
        row = i * tm
        return (jnp.where(row < geom.t_ctx, 0, ((row - geom.t_ctx) % geom.lat_len) // tm), 0)

    return pl.pallas_call(
        kern,
        grid=(geom.t // tm, N_MAIN // tn),
        in_specs=[
            pl.BlockSpec((tm, D), lambda i, j: (i, 0)),
            pl.BlockSpec((None, 6, D), lambda i, j: (geom.mod_row(i * tm), 0, 0)),
            pl.BlockSpec((1, D), lambda i, j: (0, 0)),
            pl.BlockSpec((tm, D), pos_map),
            pl.BlockSpec((D, tn), lambda i, j: (0, j)),
            pl.BlockSpec((D, LANE), lambda i, j: (0, 0)),
        ],
        out_specs=[
            pl.BlockSpec((tm, tn), lambda i, j: (i, j)),
            pl.BlockSpec((tm, LANE), lambda i, j: (i, 0)),
        ],
        out_shape=[
            jax.ShapeDtypeStruct((geom.t, N_MAIN), F32),
            jax.ShapeDtypeStruct((geom.t, LANE), F32),
        ],
        scratch_shapes=[pltpu.VMEM((tm, D), BF16)],
        compiler_params=_params(("arbitrary", "arbitrary")),
        name="in_proj",
    )(x, mod, norm_w, pos, w_main, w_dt)


def _shift_rows(x, prev_row, next_row):
    rows = x.shape[0]
    rid = lax.broadcasted_iota(jnp.int32, (rows, 1), 0)
    xm1 = jnp.where(rid == 0, prev_row, pltpu.roll(x, 1, axis=0))
    xp1 = jnp.where(rid == rows - 1, next_row, pltpu.roll(x, rows - 1, axis=0))
    return xm1, xp1


def _ssd_direction(xbc, dtraw, s_ref, y_ref, dtbias, a_row, dpar, *, lane0, reverse):
    xs = xbc[:, :SSD_INNER]
    bmat = xbc[:, SSD_INNER:SSD_INNER + GROUPS * STATE]
    cmat = xbc[:, SSD_INNER + GROUPS * STATE:]

    dt = _softplus(dtraw + dtbias)
    da = dt * a_row
    ri = lax.broadcasted_iota(jnp.int32, (CHUNK, CHUNK), 0)
    ci = lax.broadcasted_iota(jnp.int32, (CHUNK, CHUNK), 1)
    lower = ci <= ri
    upper = ci >= ri
    lower_f = jnp.where(lower, 1.0, 0.0)
    upper_f = jnp.where(upper, 1.0, 0.0)
    hp = lax.Precision.HIGHEST
    da_t = da.T
    if not reverse:
        cum = jnp.dot(lower_f, da, precision=hp, preferred_element_type=F32)
        cum_t = jnp.dot(da_t, upper_f, precision=hp, preferred_element_type=F32)
        mask = lower
        total = cum[CHUNK - 1:CHUNK, :]
    else:
        cum = jnp.dot(upper_f, da, precision=hp, preferred_element_type=F32)
        cum_t = jnp.dot(da_t, lower_f, precision=hp, preferred_element_type=F32)
        mask = upper
        total = cum[0:1, :]
    e_cum = jnp.exp(cum)
    e_tot = jnp.exp(total)
    decay = jnp.exp(total - cum)

    cb = []
    b_t = []
    c_bf = []
    for g in range(GROUPS):
        bg = bmat[:, g * STATE:(g + 1) * STATE]
        cg = cmat[:, g * STATE:(g + 1) * STATE].astype(BF16)
        c_bf.append(cg)
        cb.append(lax.dot_general(cg, bg.astype(BF16), (((1,), (1,)), ((), ())), preferred_element_type=F32))
        b_t.append(bg.T.astype(BF16))

    heads_per_group = HEADS // GROUPS
    for h in range(HEADS):
        g = h // heads_per_group
        hl = lane0 + h
        sl = slice(h * HEAD_DIM, (h + 1) * HEAD_DIM)
        col = cum[:, hl:hl + 1]
        row = cum_t[hl:hl + 1, :]
        lmat = jnp.exp(jnp.where(mask, col - row, NEG_BIG))
        scores = (cb[g] * lmat).astype(BF16)
        xh = xs[:, sl]
        xdt = xh * dt[:, hl:hl + 1]
        y = jnp.dot(scores, xdt.astype(BF16), preferred_element_type=F32)
        s_old = s_ref[:, sl]
        y = y + jnp.dot(c_bf[g], s_old.astype(BF16), preferred_element_type=F32) * e_cum[:, hl:hl + 1]
        if dpar is not None:
            y = y + dpar[:, sl] * xh
        y_ref[:, sl] = y
        st = jnp.dot(b_t[g], (xdt * decay[:, hl:hl + 1]).astype(BF16), preferred_element_type=F32)
        s_ref[:, sl] = e_tot[:, hl:hl + 1] * s_old + st


def _ssd_step_info(geom, s):
    nc_ctx = geom.ctx_len // CHUNK
    nc_lat = geom.lat_len // CHUNK
    n_ctx_chunks = geom.n_ctx * nc_ctx
    is_ctx = s < n_ctx_chunks
    c = jnp.where(is_ctx, s % nc_ctx, (s - n_ctx_chunks) % nc_lat)
    nc = jnp.where(is_ctx, nc_ctx, nc_lat)
    seq = jnp.where(is_ctx, s // nc_ctx, geom.n_ctx + (s - n_ctx_chunks) // nc_lat)
    return is_ctx, c, nc, seq


def _ssd_kernel(xf_ref, xfp_ref, xfn_ref, dtf_ref, xb_ref, xbp_ref, xbn_ref, dtb_ref, h0_ref,
                cw_ref, cb_ref, dtbias_ref, arow_ref, dpar_ref,
                yf_ref, yr_ref, st_ref, s_ref, *, geom):
    is_ctx, c, nc, _ = _ssd_step_info(geom, pl.program_id(0))
    first = c == 0
    last = c == nc - 1

    @pl.when(first)
    def _():
        keep = jnp.where(is_ctx, 0.0, 1.0)
        for d in range(2):
            h0 = h0_ref[d].reshape(HEADS * HEAD_DIM, STATE)
            s_ref[d] = h0.T * keep

    cw = cw_ref[...]
    cbias = cb_ref[...]
    not_first = jnp.where(first, 0.0, 1.0)
    not_last = jnp.where(last, 0.0, 1.0)

    def conv(x_ref, p_ref, n_ref, has_prev, has_next):
        x = x_ref[...]
        xm1, xp1 = _shift_rows(x, p_ref[SUBLANE - 1:SUBLANE, :] * has_prev, n_ref[0:1, :] * has_next)
        return _silu(cw[0:1] * xm1 + cw[1:2] * x + cw[2:3] * xp1 + cbias)

    xbc_f = conv(xf_ref, xfp_ref, xfn_ref, not_first, not_last)
    _ssd_direction(xbc_f, dtf_ref[...], s_ref.at[0], yf_ref, dtbias_ref[...], arow_ref[...], dpar_ref[...],
                   lane0=0, reverse=False)
    xbc_b = conv(xb_ref, xbp_ref, xbn_ref, not_last, not_first)
    _ssd_direction(xbc_b, dtb_ref[...], s_ref.at[1], yr_ref, dtbias_ref[...], arow_ref[...], None,
                   lane0=HEADS, reverse=True)

    @pl.when(last & is_ctx)
    def _():
        for d in range(2):
            st_ref[d] = s_ref[d].T.reshape(HEADS, HEAD_DIM, STATE)


def _ssd(geom, layer, proj, dt_raw, state_ssd, conv_w, conv_b, dtbias_row, a_row, dpar_row):
    n_chunks = geom.t // CHUNK
    per = CHUNK // SUBLANE
    last_halo = geom.t // SUBLANE - 1
    xblk = COL_XBC // XBC

    def fwd_chunk(s):
        return s

    def bwd_chunk(s):
        _, c, nc, _ = _ssd_step_info(geom, s)
        return s + nc - 1 - 2 * c

    def chunk_specs(chunk_fn):
        return [
            pl.BlockSpec((CHUNK, XBC), lambda s: (chunk_fn(s), xblk)),
            pl.BlockSpec((SUBLANE, XBC), lambda s: (jnp.maximum(chunk_fn(s) * per - 1, 0), xblk)),
            pl.BlockSpec((SUBLANE, XBC), lambda s: (jnp.minimum((chunk_fn(s) + 1) * per, last_halo), xblk)),
            pl.BlockSpec((CHUNK, LANE), lambda s: (chunk_fn(s), 0)),
        ]

    def h0_map(s):
        _, _, _, seq = _ssd_step_info(geom, s)
        return (jnp.maximum(seq - geom.n_ctx, 0), layer, 0, 0, 0, 0)

    def st_map(s):
        _, _, _, seq = _ssd_step_info(geom, s)
        return (jnp.minimum(seq, geom.n_ctx - 1), 0, 0, 0, 0)

    const2 = lambda s: (0, 0)
    state_blk = (None, None, 2, HEADS, HEAD_DIM, STATE)
    return pl.pallas_call(
        functools.partial(_ssd_kernel, geom=geom),
        grid=(n_chunks,),
        in_specs=chunk_specs(fwd_chunk) + [proj_spec for proj_spec in chunk_specs(bwd_chunk)] + [
            pl.BlockSpec(state_blk, h0_map),
            pl.BlockSpec((3, XBC), const2),
            pl.BlockSpec((1, XBC), const2),
            pl.BlockSpec((1, LANE), const2),
            pl.BlockSpec((1, LANE), const2),
            pl.BlockSpec((1, SSD_INNER), const2),
        ],
        out_specs=[
            pl.BlockSpec((CHUNK, SSD_INNER), lambda s: (fwd_chunk(s), 0)),
            pl.BlockSpec((CHUNK, SSD_INNER), lambda s: (bwd_chunk(s), 0)),
            pl.BlockSpec((None, 2, HEADS, HEAD_DIM, STATE), st_map),
        ],
        out_shape=[
            jax.ShapeDtypeStruct((geom.t, SSD_INNER), F32),
            jax.ShapeDtypeStruct((geom.t, SSD_INNER), F32),
            jax.ShapeDtypeStruct((geom.n_ctx, 2, HEADS, HEAD_DIM, STATE), F32),
        ],
        scratch_shapes=[pltpu.VMEM((2, STATE, HEADS * HEAD_DIM), F32)],
        compiler_params=_params(("arbitrary",)),
        name="ssd",
    )(proj, proj, proj, dt_raw, proj, proj, proj, dt_raw, state_ssd,
      conv_w, conv_b, dtbias_row, a_row, dpar_row)


def _branch_kernel(ab_ref, ac_ref, ah_ref, acp_ref, acn_ref, ahp_ref, ahn_ref, z_ref, yf_ref, yr_ref,
                   su_ref, sv_ref, caw_ref, nw_ref, lng_ref, lnb_ref, sw_ref, sb_ref,
                   ya_ref, ys_ref, yc_ref, *, geom, tm):
    i = pl.program_id(0)
    has_prev, has_next = geom.halo_flags(i * tm, tm)

    u = ac_ref[...] * ah_ref[...]
    u_prev = acp_ref[SUBLANE - 1:SUBLANE, :] * ahp_ref[SUBLANE - 1:SUBLANE, :] * has_prev
    u_next = acn_ref[0:1, :] * ahn_ref[0:1, :] * has_next
    um1, up1 = _shift_rows(u, u_prev, u_next)
    caw = caw_ref[...]
    ya_ref[...] = (ab_ref[...] * (caw[0:1] * um1 + caw[1:2] * u + caw[2:3] * up1)).astype(BF16)

    y = (yf_ref[...] + yr_ref[...]) * _silu(z_ref[...])
    gw = SSD_INNER // GROUPS
    for g in range(GROUPS):
        sl = slice(g * gw, (g + 1) * gw)
        seg = y[:, sl]
        r = lax.rsqrt(jnp.mean(seg * seg, axis=-1, keepdims=True) + EPS)
        ys_ref[:, sl] = (seg * r * nw_ref[:, sl]).astype(BF16)

    uu = _gelu_tanh(su_ref[...])
    v = _gelu_tanh(sv_ref[...])
    mu = jnp.mean(v, axis=-1, keepdims=True)
    vc = v - mu
    var = jnp.mean(vc * vc, axis=-1, keepdims=True)
    v = (vc * lax.rsqrt(var + EPS) * lng_ref[...] + lnb_ref[...]).astype(BF16)
    sb = sb_ref[...]
    cw = SGU // SGU_GROUPS
    for q in range(tm // CHUNK):
        rs = slice(q * CHUNK, (q + 1) * CHUNK)
        for g in range(SGU_GROUPS):
            cs = slice(g * cw, (g + 1) * cw)
            sp = jnp.dot(sw_ref[g], v[rs, cs], preferred_element_type=F32) + sb[:, g:g + 1]
            yc_ref[rs, cs] = (uu[rs, cs] * sp).astype(BF16)


def _branches(geom, proj, y_f, y_r, conv_a_w, ssd_norm_w, ln_g, ln_b, sgu_w_bf, sgu_b):
    tm = 256
    assert geom.ctx_len % tm == 0 and geom.lat_len % tm == 0
    blk1k = lambda col: pl.BlockSpec((tm, 1024), lambda i: (i, col // 1024))
    acp, acn = geom.halo_specs(tm, 1024, COL_AC // 1024)
    ahp, ahn = geom.halo_specs(tm, 1024, COL_AH // 1024)
    const2 = lambda i: (0, 0)
    out_bf = lambda w: jax.ShapeDtypeStruct((geom.t, w), BF16)
    return pl.pallas_call(
        functools.partial(_branch_kernel, geom=geom, tm=tm),
        grid=(geom.t // tm,),
        in_specs=[
            blk1k(COL_AB), blk1k(COL_AC), blk1k(COL_AH), acp, acn, ahp, ahn,
            pl.BlockSpec((tm, SSD_INNER), lambda i: (i, COL_Z // SSD_INNER)),
            pl.BlockSpec((tm, SSD_INNER), lambda i: (i, 0)),
            pl.BlockSpec((tm, SSD_INNER), lambda i: (i, 0)),
            blk1k(COL_SU), blk1k(COL_SV),
            pl.BlockSpec((3, CONV_A), const2),
            pl.BlockSpec((1, SSD_INNER), const2),
            pl.BlockSpec((1, SGU), const2),
            pl.BlockSpec((1, SGU), const2),
            pl.BlockSpec((SGU_GROUPS, CHUNK, CHUNK), lambda i: (0, 0, 0)),
            pl.BlockSpec((CHUNK, SGU_GROUPS), const2),
        ],
        out_specs=[
            pl.BlockSpec((tm, CONV_A), lambda i: (i, 0)),
            pl.BlockSpec((tm, SSD_INNER), lambda i: (i, 0)),
            pl.BlockSpec((tm, SGU), lambda i: (i, 0)),
        ],
        out_shape=[out_bf(CONV_A), out_bf(SSD_INNER), out_bf(SGU)],
        compiler_params=_params(("arbitrary",)),
        name="branches",
    )(proj, proj, proj, proj, proj, proj, proj, proj, y_f, y_r, proj, proj,
      conv_a_w, ssd_norm_w, ln_g, ln_b, sgu_w_bf, sgu_b)


def _merge_kernel(ya_ref, ys_ref, yc_ref, ga_ref, gb_ref, gc_ref, wa_ref, wb_ref, wc_ref, wo_ref,
                  x_ref, mod_ref, o_ref, acc_ref):
    j = pl.program_id(1)

    @pl.when(j == 0)
    def _():
        acc_ref[...] = jnp.zeros_like(acc_ref)

    m = _sigmoid(ga_ref[...]) * jnp.dot(ya_ref[...], wa_ref[...], preferred_element_type=F32)
    m = m + _sigmoid(gb_ref[...]) * jnp.dot(ys_ref[...], wb_ref[...], preferred_element_type=F32)
    m = m + _sigmoid(gc_ref[...]) * jnp.dot(yc_ref[...], wc_ref[...], preferred_element_type=F32)
    acc_ref[...] += jnp.dot(m.astype(BF16), wo_ref[...], preferred_element_type=F32)

    @pl.when(j == pl.num_programs(1) - 1)
    def _():
        o_ref[...] = x_ref[...] + mod_ref[2:3, :] * acc_ref[...]


def _merge(geom, ya, ys, yc, proj, wa, wb, wc, wo, x, mod):
    tm, tn = 512, 512
    assert geom.t_ctx % tm == 0 and geom.lat_len % tm == 0
    gate = lambda k: pl.BlockSpec((tm, tn), lambda i, j: (i, (COL_GATE + k * D) // tn + j))
    return pl.pallas_call(
        _merge_kernel,
        grid=(geom.t // tm, D // tn),
        in_specs=[
            pl.BlockSpec((tm, CONV_A), lambda i, j: (i, 0)),
            pl.BlockSpec((tm, SSD_INNER), lambda i, j: (i, 0)),
            pl.BlockSpec((tm, SGU), lambda i, j: (i, 0)),
            gate(0), gate(1), gate(2),
            pl.BlockSpec((CONV_A, tn), lambda i, j: (0, j)),
            pl.BlockSpec((SSD_INNER, tn), lambda i, j: (0, j)),
            pl.BlockSpec((SGU, tn), lambda i, j: (0, j)),
            pl.BlockSpec((tn, D), lambda i, j: (j, 0)),
            pl.BlockSpec((tm, D), lambda i, j: (i, 0)),
            pl.BlockSpec((None, 6, D), lambda i, j: (geom.mod_row(i * tm), 0, 0)),
        ],
        out_specs=pl.BlockSpec((tm, D), lambda i, j: (i, 0)),
        out_shape=jax.ShapeDtypeStruct((geom.t, D), F32),
        scratch_shapes=[pltpu.VMEM((tm, D), F32)],
        compiler_params=_params(("arbitrary", "arbitrary")),
        name="merge",
    )(ya, ys, yc, proj, proj, proj, wa, wb, wc, wo, x, mod)


def _router_kernel(x_ref, mod_ref, nw_ref, wr_ref, br_ref, h_ref, comb_ref):
    x = x_ref[...]
    ms = jnp.mean(x * x, axis=-1, keepdims=True)
    y = x * lax.rsqrt(ms + EPS) * nw_ref[...]
    h = y * (1.0 + mod_ref[4:5, :]) + mod_ref[3:4, :]
    h_ref[...] = h.astype(BF16)

    logits = jnp.dot(h, wr_ref[...], precision=lax.Precision.HIGHEST, preferred_element_type=F32) + br_ref[...]
    lg = [logits[:, k:k + 1] for k in range(E_GROUPS)]
    le = [logits[:, E_GROUPS + k:E_GROUPS + k + 1] for k in range(N_EXP)]

    mg = functools.reduce(jnp.maximum, lg)
    eg = [jnp.exp(v - mg) for v in lg]
    sg = functools.reduce(lambda a, b: a + b, eg)
    pg = [v / sg for v in eg]
    gi = jnp.zeros_like(lg[0]).astype(jnp.int32)
    best = pg[0]
    for k in range(1, E_GROUPS):
        better = pg[k] > best
        gi = jnp.where(better, k, gi)
        best = jnp.where(better, pg[k], best)

    ls = []
    for k in range(EXP_PER_GROUP):
        v = le[k]
        for g in range(1, E_GROUPS):
            v = jnp.where(gi == g, le[g * EXP_PER_GROUP + k], v)
        ls.append(v)
    ml = functools.reduce(jnp.maximum, ls)
    el = [jnp.exp(v - ml) for v in ls]
    sl = functools.reduce(lambda a, b: a + b, el)
    pe = [v / sl for v in el]

    i1 = jnp.zeros_like(gi)
    v1 = pe[0]
    for k in range(1, EXP_PER_GROUP):
        better = pe[k] > v1
        i1 = jnp.where(better, k, i1)
        v1 = jnp.where(better, pe[k], v1)
    i2 = jnp.full_like(gi, -1)
    v2 = jnp.full_like(v1, -1.0)
    for k in range(EXP_PER_GROUP):
        better = (i1 != k) & (pe[k] > v2)
        i2 = jnp.where(better, k, i2)
        v2 = jnp.where(better, pe[k], v2)
    tot = v1 + v2
    w1 = best * v1 / tot
    w2 = best * v2 / tot

    lane = lax.broadcasted_iota(jnp.int32, comb_ref.shape, 1)
    e1 = gi * EXP_PER_GROUP + i1
    e2 = gi * EXP_PER_GROUP + i2
    comb_ref[...] = jnp.where(lane == e1, w1, 0.0) + jnp.where(lane == e2, w2, 0.0)


def _router(geom, x, mod, norm_w, w_r, b_r):
    tm = 256
    assert geom.t_ctx % tm == 0 and geom.lat_len % tm == 0
    return pl.pallas_call(
        _router_kernel,
        grid=(geom.t // tm,),
        in_specs=[
            pl.BlockSpec((tm, D), lambda i: (i, 0)),
            pl.BlockSpec((None, 6, D), lambda i: (geom.mod_row(i * tm), 0, 0)),
            pl.BlockSpec((1, D), lambda i: (0, 0)),
            pl.BlockSpec((D, LANE), lambda i: (0, 0)),
            pl.BlockSpec((1, LANE), lambda i: (0, 0)),
        ],
        out_specs=[
            pl.BlockSpec((tm, D), lambda i: (i, 0)),
            pl.BlockSpec((tm, LANE), lambda i: (i, 0)),
        ],
        out_shape=[
            jax.ShapeDtypeStruct((geom.t, D), BF16),
            jax.ShapeDtypeStruct((geom.t, LANE), F32),
        ],
        compiler_params=_params(("arbitrary",)),
        name="router",
    )(x, mod, norm_w, w_r, b_r)


def _moe_kernel(h_ref, comb_ref, wg_ref, wu_ref, wd_ref, o_ref):
    e = pl.program_id(1)

    @pl.when(e == 0)
    def _():
        o_ref[...] = jnp.zeros_like(o_ref)

    h = h_ref[...]
    hg = jnp.dot(h, wg_ref[...], preferred_element_type=F32)
    hu = jnp.dot(h, wu_ref[...], preferred_element_type=F32)
    comb = comb_ref[...]
    lane = lax.broadcasted_iota(jnp.int32, comb.shape, 1)
    ce = jnp.sum(jnp.where(lane == e, comb, 0.0), axis=-1, keepdims=True)
    act = (_silu(hg) * hu * ce).astype(BF16)
    o_ref[...] += jnp.dot(act, wd_ref[...], preferred_element_type=F32)


def _moe(geom, h, comb, wg, wu, wd):
    tm = 1024 if geom.t % 1024 == 0 else 512
    return pl.pallas_call(
        _moe_kernel,
        grid=(geom.t // tm, N_EXP),
        in_specs=[
            pl.BlockSpec((tm, D), lambda i, e: (i, 0)),
            pl.BlockSpec((tm, LANE), lambda i, e: (i, 0)),
            pl.BlockSpec((None, D, D_EXPERT), lambda i, e: (e, 0, 0)),
            pl.BlockSpec((None, D, D_EXPERT), lambda i, e: (e, 0, 0)),
            pl.BlockSpec((None, D_EXPERT, D), lambda i, e: (e, 0, 0)),
        ],
        out_specs=pl.BlockSpec((tm, D), lambda i, e: (i, 0)),
        out_shape=jax.ShapeDtypeStruct((geom.t, D), F32),
        compiler_params=_params(("arbitrary", "arbitrary")),
        name="moe",
    )(h, comb, wg, wu, wd)


def _residual_kernel(x_ref, y_ref, mod_ref, o_ref):
    o_ref[...] = x_ref[...] + mod_ref[5:6, :] * y_ref[...]


def _residual_norm_kernel(x_ref, y_ref, mod_ref, nw_ref, o_ref):
    x = x_ref[...] + mod_ref[5:6, :] * y_ref[...]
    ms = jnp.mean(x * x, axis=-1, keepdims=True)
    o_ref[...] = x * lax.rsqrt(ms + EPS) * nw_ref[...]


def _residual(geom, x, y, mod, final_w=None):
    tm = 512
    row = pl.BlockSpec((tm, D), lambda i: (i, 0))
    in_specs = [row, row, pl.BlockSpec((None, 6, D), lambda i: (geom.mod_row(i * tm), 0, 0))]
    args = [x, y, mod]
    kern = _residual_kernel
    if final_w is not None:
        in_specs.append(pl.BlockSpec((1, D), lambda i: (0, 0)))
        args.append(final_w)
        kern = _residual_norm_kernel
    return pl.pallas_call(
        kern,
        grid=(geom.t // tm,),
        in_specs=in_specs,
        out_specs=row,
        out_shape=jax.ShapeDtypeStruct((geom.t, D), F32),
        compiler_params=_params(("arbitrary",)),
        name="residual",
    )(*args)


def _grid_pos_code(n_tok):
    rows = n_tok // GRID_W
    rr, cc = jnp.meshgrid(jnp.arange(rows, dtype=F32), jnp.arange(GRID_W, dtype=F32), indexing='ij')
    quarter = D // 4
    omega = 1.0 / (10000.0 ** (jnp.arange(quarter, dtype=F32) / quarter))
    ar = rr.reshape(-1)[:, None] * omega
    ac = cc.reshape(-1)[:, None] * omega
    return jnp.concatenate([jnp.sin(ar), jnp.cos(ar), jnp.sin(ac), jnp.cos(ac)], axis=-1)


def _regroup_w_in(w):
    a_b, a_c, a_h = w[:, 0:1024], w[:, 1024:2048], w[:, 2048:3072]
    z = w[:, 3072:5120]
    xbc = w[:, 5120:8192]
    dt = w[:, 8192:8256]
    s_u, s_v = w[:, 8256:9280], w[:, 9280:10304]
    gates = w[:, 10304:]
    main = jnp.concatenate([xbc, a_b, a_c, a_h, z, s_u, s_v, gates], axis=1).astype(BF16)
    dtp = jnp.pad(dt, ((0, 0), (0, LANE - dt.shape[1]))).astype(BF16)
    return main, dtp


def _row_pad(v, width=LANE):
    v = v.reshape(1, -1)
    return jnp.pad(v, ((0, 0), (0, width - v.shape[1])))


def kernel(x_prompt, x_sample, state_ssd, c, c_ctx, norm_mix_w, norm_ffn_w, w_ada, b_ada, w_in, conv_a_w, w_out_a, ssd_conv_w, ssd_conv_b, ssd_a_log, ssd_dt_bias, ssd_d, ssd_norm_w, w_out_b, sgu_ln_g, sgu_ln_b, sgu_w, sgu_b, w_out_c, w_o, router_g_w, router_g_b, router_e_w, router_e_b, exp_w_gate, exp_w_up, exp_w_down, final_norm_w):
    n_ctx, ctx_len, _ = x_prompt.shape
    n_lat, lat_len, _ = x_sample.shape
    depth = w_in.shape[0]
    geom = _Geom(n_ctx, ctx_len, n_lat, lat_len)

    x = jnp.concatenate([x_prompt.reshape(-1, D), x_sample.reshape(-1, D)], axis=0)
    pos = _grid_pos_code(lat_len)

    c_all = jnp.concatenate([c_ctx[None, :], c, jnp.zeros((SUBLANE - 1 - n_lat, D), F32)], axis=0)
    mod_all = _ada(c_all, w_ada, b_ada).reshape(depth, SUBLANE, 6, D)

    states = []
    for l in range(depth):
        mod = mod_all[l]
        w_main, w_dt = _regroup_w_in(w_in[l])
        proj, dt_raw = _in_proj(geom, x, mod, norm_mix_w[l][None, :], pos, w_main, w_dt)

        dtbias_row = _row_pad(ssd_dt_bias[l])
        a_row = _row_pad(-jnp.exp(ssd_a_log[l]))
        dpar_row = jnp.repeat(ssd_d[l], HEAD_DIM)[None, :]
        y_f, y_r, st = _ssd(geom, l, proj, dt_raw, state_ssd, ssd_conv_w[l], ssd_conv_b[l][None, :],
                            dtbias_row, a_row, dpar_row)
        states.append(st)

        ya, ys, yc = _branches(geom, proj, y_f, y_r, conv_a_w[l], ssd_norm_w[l][None, :],
                               sgu_ln_g[l][None, :], sgu_ln_b[l][None, :], sgu_w[l].astype(BF16), sgu_b[l])
        x = _merge(geom, ya, ys, yc, proj, w_out_a[l].astype(BF16), w_out_b[l].astype(BF16),
                   w_out_c[l].astype(BF16), w_o[l].astype(BF16), x, mod)

        w_r = jnp.pad(jnp.concatenate([router_g_w[l], router_e_w[l]], axis=1),
                      ((0, 0), (0, LANE - E_GROUPS - N_EXP)))
        b_r = _row_pad(jnp.concatenate([router_g_b[l], router_e_b[l]]))
        h2, comb = _router(geom, x, mod, norm_ffn_w[l][None, :], w_r, b_r)
        y = _moe(geom, h2, comb, exp_w_gate[l].astype(BF16), exp_w_up[l].astype(BF16),
                 exp_w_down[l].astype(BF16))
        x = _residual(geom, x, y, mod, final_norm_w[None, :] if l == depth - 1 else None)

    y_prompt = x[:geom.t_ctx].reshape(n_ctx, ctx_len, D)
    y_sample = x[geom.t_ctx:].reshape(n_lat, lat_len, D)
    new_state = jnp.stack(states, axis=1)
    return (y_prompt, y_sample, new_state)
```
